```python
import jax, jax.numpy as jnp
from jax import lax
import numpy as np

D_MODEL = 1024
BATCH = 4
SEQ = 8192
DEPTH = 2

CHUNK = 64
D_MIX = D_MODEL
A_WIDTH = D_MIX // 4
A_HEADS = 4
A_HEAD_DIM = A_WIDTH // A_HEADS
A_BLOCK = 128
POOL_WINDOWS = (2, 4, 8, 16)
B_WIDTH = D_MIX // 4
B_GROUP_DIM = B_WIDTH // len(POOL_WINDOWS)
C_WIDTH = D_MIX - A_WIDTH - B_WIDTH
C_HEADS = 8
C_HEAD_DIM = C_WIDTH // C_HEADS
ROPE_BASE = 10000.0
IN_COLS = 2 * A_WIDTH + B_WIDTH + 4 * C_WIDTH
D_FF = ((8 * D_MODEL // 3 + 127) // 128) * 128
CONV_WIDTH = 3
EPS = 1e-6

kernel_name = "hybrid_gmlp_pool_retention_convffn"


def rms_norm(x, g):
    xf = x.astype(jnp.float32)
    y = xf * lax.rsqrt(jnp.mean(xf * xf, axis=-1, keepdims=True) + EPS)
    return (y * g.astype(jnp.float32)).astype(x.dtype)


def spatial_gating(z, vnorm_g, ws, bs):
    bsz, s_len, _ = z.shape
    u, v = jnp.split(z, 2, axis=-1)
    v = rms_norm(v, vnorm_g)
    nb = s_len // A_BLOCK
    v = v.reshape(bsz, nb, A_BLOCK, A_HEADS, A_HEAD_DIM)
    chunk_id = jnp.arange(A_BLOCK) // CHUNK
    mask = chunk_id[:, None] >= chunk_id[None, :]
    w = jnp.where(mask[None], ws, jnp.zeros_like(ws))
    sv = jnp.einsum('hts,bnshd->bnthd', w, v) + bs.T[None, None, :, :, None]
    return u * sv.reshape(bsz, s_len, A_WIDTH)


def multiscale_pool(xb, w_grp, scale):
    bsz, s_len, _ = xb.shape
    xf = xb.astype(jnp.float32)
    cs = jnp.concatenate([jnp.zeros((bsz, 1, B_WIDTH), jnp.float32), jnp.cumsum(xf, axis=1)], axis=1)
    t = jnp.arange(s_len)
    outs = []
    for gi, win in enumerate(POOL_WINDOWS):
        sl = slice(gi * B_GROUP_DIM, (gi + 1) * B_GROUP_DIM)
        lo = jnp.maximum(t + 1 - win, 0)
        cnt = (t + 1 - lo).astype(jnp.float32)
        outs.append((cs[:, 1:, sl] - cs[:, lo, sl]) / cnt[None, :, None])
    pooled = jnp.concatenate(outs, axis=-1).astype(xb.dtype) - xb
    pooled = pooled.reshape(bsz, s_len, len(POOL_WINDOWS), B_GROUP_DIM)
    y = jnp.einsum('bsgc,gcd->bsgd', pooled, w_grp).reshape(bsz, s_len, B_WIDTH)
    return y * scale


def rotary(x, pos):
    half = x.shape[-1] // 2
    inv = ROPE_BASE ** (-jnp.arange(half, dtype=jnp.float32) / half)
    ang = pos.astype(jnp.float32)[:, None] * inv[None, :]
    cos = jnp.cos(ang)[None, :, None, :]
    sin = jnp.sin(ang)[None, :, None, :]
    xf = x.astype(jnp.float32)
    x1, x2 = xf[..., :half], xf[..., half:]
    return jnp.concatenate([x1 * cos - x2 * sin, x2 * cos + x1 * sin], axis=-1).astype(x.dtype)


def retention(q, k, v, g, norm_g):
    bsz, s_len, _ = q.shape
    n_chunks = s_len // CHUNK
    dt = q.dtype
    pos = jnp.arange(s_len)
    q = rotary(q.reshape(bsz, s_len, C_HEADS, C_HEAD_DIM), pos) * (C_HEAD_DIM ** -0.5)
    k = rotary(k.reshape(bsz, s_len, C_HEADS, C_HEAD_DIM), pos)
    v = v.reshape(bsz, s_len, C_HEADS, C_HEAD_DIM)
    log_gamma = jnp.log1p(-jnp.exp2(-5.0 - jnp.arange(C_HEADS, dtype=jnp.float32)))
    idx = jnp.arange(CHUNK, dtype=jnp.float32)
    d_intra = jnp.exp(log_gamma[:, None, None] * jnp.abs(idx[:, None] - idx[None, :])).astype(dt)
    k_dec = jnp.exp(log_gamma[None, :] * (CHUNK - 1 - idx)[:, None]).astype(dt)
    q_dec = jnp.exp(log_gamma[None, :] * (idx + 1)[:, None]).astype(dt)
    chunk_dec = jnp.exp(log_gamma * CHUNK).astype(dt)
    qc = q.reshape(bsz, n_chunks, CHUNK, C_HEADS, C_HEAD_DIM)
    kc = k.reshape(bsz, n_chunks, CHUNK, C_HEADS, C_HEAD_DIM)
    vc = v.reshape(bsz, n_chunks, CHUNK, C_HEADS, C_HEAD_DIM)
    scores = jnp.einsum('bnthd,bnshd->bnhts', qc, kc) * d_intra
    y_intra = jnp.einsum('bnhts,bnshe->bnthe', scores, vc)
    kv = jnp.einsum('bnshd,bnshe->nbhde', kc * k_dec[:, :, None], vc)

    def step(state, kv_n):
        return state * chunk_dec[None, :, None, None] + kv_n, state

    _, s_prev = lax.scan(step, jnp.zeros(kv.shape[1:], kv.dtype), kv)
    y_cross = jnp.einsum('bnthd,nbhde->bnthe', qc * q_dec[:, :, None], s_prev)
    y = (y_intra + y_cross).reshape(bsz, s_len, C_HEADS, C_HEAD_DIM)
    yf = y.astype(jnp.float32)
    mu = jnp.mean(yf, axis=-1, keepdims=True)
    var = jnp.mean(jnp.square(yf - mu), axis=-1, keepdims=True)
    yf = (yf - mu) * lax.rsqrt(var + EPS)
    y = (yf.reshape(bsz, s_len, C_WIDTH) * norm_g.astype(jnp.float32)).astype(dt)
    return jax.nn.silu(g) * y


def conv_ffn(h, w_up, conv_w, conv_b, w_down):
    up = h @ w_up
    s_len = up.shape[1]
    padded = jnp.pad(up, ((0, 0), (CONV_WIDTH - 1, 0), (0, 0)))
    conv = conv_b + padded[:, 0:s_len] * conv_w[0]
    for j in range(1, CONV_WIDTH):
        conv = conv + padded[:, j:j + s_len] * conv_w[j]
    gate, val = jnp.split(conv, 2, axis=-1)
    return (jax.nn.silu(gate) * val) @ w_down


def setup_inputs(seed: int = 0) -> dict:
    key = jax.random.key(seed)
    ks = jax.random.split(key, 20)
    f32 = jnp.float32
    nrm = lambda k, shape, s: jax.random.normal(k, shape, f32) * s
    return {
        "x": jax.random.normal(ks[0], (BATCH, SEQ, D_MODEL), f32),
        "norm1_g": 1.0 + nrm(ks[1], (DEPTH, D_MODEL), 0.05),
        "w_in": nrm(ks[2], (DEPTH, D_MODEL, IN_COLS), D_MODEL ** -0.5),
        "a_vnorm_g": 1.0 + nrm(ks[3], (DEPTH, A_WIDTH), 0.05),
        "a_ws": nrm(ks[4], (DEPTH, A_HEADS, A_BLOCK, A_BLOCK), 0.5 * A_BLOCK ** -0.5),
        "a_bs": 1.0 + nrm(ks[5], (DEPTH, A_HEADS, A_BLOCK), 0.1),
        "b_w": nrm(ks[6], (DEPTH, len(POOL_WINDOWS), B_GROUP_DIM, B_GROUP_DIM), B_GROUP_DIM ** -0.5),
        "b_scale": 1.0 + nrm(ks[7], (DEPTH, B_WIDTH), 0.1),
        "c_norm_g": 1.0 + nrm(ks[8], (DEPTH, C_WIDTH), 0.05),
        "w_out": nrm(ks[9], (DEPTH, D_MIX, D_MODEL), D_MIX ** -0.5),
        "norm2_g": 1.0 + nrm(ks[10], (DEPTH, D_MODEL), 0.05),
        "w_up": nrm(ks[11], (DEPTH, D_MODEL, 2 * D_FF), D_MODEL ** -0.5),
        "conv_w": nrm(ks[12], (DEPTH, CONV_WIDTH, 2 * D_FF), CONV_WIDTH ** -0.5),
        "conv_b": nrm(ks[13], (DEPTH, 2 * D_FF), 0.02),
        "w_down": nrm(ks[14], (DEPTH, D_FF, D_MODEL), D_FF ** -0.5),
        "final_g": 1.0 + nrm(ks[15], (D_MODEL,), 0.05),
    }


def reference(x, norm1_g, w_in, a_vnorm_g, a_ws, a_bs, b_w, b_scale, c_norm_g,
              w_out, norm2_g, w_up, conv_w, conv_b, w_down, final_g):
    splits = [2 * A_WIDTH, 2 * A_WIDTH + B_WIDTH, 2 * A_WIDTH + B_WIDTH + C_WIDTH,
              2 * A_WIDTH + B_WIDTH + 2 * C_WIDTH, 2 * A_WIDTH + B_WIDTH + 3 * C_WIDTH]
    for l in range(DEPTH):
        h = rms_norm(x, norm1_g[l])
        proj = h @ w_in[l]
        za, xb, q, k, v, g = jnp.split(proj, splits, axis=-1)
        ya = spatial_gating(jax.nn.gelu(za), a_vnorm_g[l], a_ws[l], a_bs[l])
        yb = multiscale_pool(xb, b_w[l], b_scale[l])
        yc = retention(q, k, v, g, c_norm_g[l])
        x = x + jnp.concatenate([ya, yb, yc], axis=-1) @ w_out[l]
        x = x + conv_ffn(rms_norm(x, norm2_g[l]), w_up[l], conv_w[l], conv_b[l], w_down[l])
    return rms_norm(x, final_g)
```

```python
import functools

import numpy as np
import jax
import jax.numpy as jnp
from jax import lax
from jax.experimental import pallas as pl
from jax.experimental.pallas import tpu as pltpu

D_MODEL = 1024
SEQ = 8192
CHUNK = 64
A_WIDTH = 256
A_HEADS = 4
A_HEAD_DIM = A_WIDTH // A_HEADS
A_BLOCK = 128
POOL_WINDOWS = (2, 4, 8, 16)
B_WIDTH = 256
B_GROUP_DIM = B_WIDTH // len(POOL_WINDOWS)
C_WIDTH = 512
C_HEADS = 8
C_HEAD_DIM = C_WIDTH // C_HEADS
ROPE_HALF = C_HEAD_DIM // 2
ROPE_BASE = 10000.0
IN_COLS = 2 * A_WIDTH + B_WIDTH + 4 * C_WIDTH
D_FF = 2816
CONV_WIDTH = 3
EPS = 1e-6

HEAD_GROUP = 4
GROUP_LANES = HEAD_GROUP * C_HEAD_DIM
N_GROUPS = C_HEADS // HEAD_GROUP
POOL_HALO = 16
CONV_HALO = 8

TILE_MIX = 512
TILE_FFN = 512
FF_CHUNK = 256
VMEM_LIMIT_BYTES = 56 * 1024 * 1024

F32 = jnp.float32
BF16 = jnp.bfloat16


def _rms(x, g):
    return x * lax.rsqrt(jnp.mean(x * x, axis=-1, keepdims=True) + EPS) * g


def _qk_lane_head(lane):
    return (lane % (GROUP_LANES // 2)) // ROPE_HALF


def _in_proj_column_order():
    cols = list(range(2 * A_WIDTH + B_WIDTH))
    base = 2 * A_WIDTH + B_WIDTH
    for gi in range(N_GROUPS):
        for part in range(4):
            start = base + part * C_WIDTH + gi * GROUP_LANES
            if part < 2:
                for lane in range(GROUP_LANES):
                    head = _qk_lane_head(lane)
                    second = lane // (GROUP_LANES // 2)
                    cols.append(start + head * C_HEAD_DIM + second * ROPE_HALF + lane % ROPE_HALF)
            else:
                cols.extend(range(start, start + GROUP_LANES))
    return np.asarray(cols, np.int32)


def _retention_constants():
    log_gamma = np.log1p(-np.exp2(-5.0 - np.arange(C_HEADS, dtype=np.float64)))
    idx = np.arange(CHUNK, dtype=np.float64)
    lane = np.arange(GROUP_LANES)
    d4 = np.zeros((N_GROUPS, CHUNK, GROUP_LANES))
    qdec = np.zeros((N_GROUPS, CHUNK, GROUP_LANES))
    kdec = np.zeros((N_GROUPS, CHUNK, GROUP_LANES))
    cdec = np.zeros((N_GROUPS, 1, GROUP_LANES))
    for gi in range(N_GROUPS):
        lg_v = log_gamma[gi * HEAD_GROUP + lane // C_HEAD_DIM]
        lg_qk = log_gamma[gi * HEAD_GROUP + _qk_lane_head(lane)]
        d4[gi] = np.exp(lg_v[None, :] * np.abs(idx[:, None] - (lane % CHUNK)[None, :]))
        qdec[gi] = np.exp(lg_qk[None, :] * (idx + 1)[:, None])
        kdec[gi] = np.exp(lg_qk[None, :] * (CHUNK - 1 - idx)[:, None])
        cdec[gi] = np.exp(lg_v * CHUNK)[None, :]
    rows_head = np.arange(GROUP_LANES) // CHUNK
    mask_k = (rows_head[:, None] == _qk_lane_head(lane)[None, :]).astype(np.float32)
    mask_v = (rows_head[:, None] == (lane // C_HEAD_DIM)[None, :]).astype(np.float32)
    f = lambda a: jnp.asarray(a, F32)
    return f(d4), f(qdec), f(kdec), f(cdec), f(mask_k), f(mask_v)


def _spatial_constants():
    chunk_id = np.arange(A_BLOCK) // CHUNK
    causal = (chunk_id[:, None] >= chunk_id[None, :]).astype(np.float32)
    smask = np.tile(causal, (1, A_HEADS))
    rows_head = np.arange(A_HEADS * A_BLOCK) // A_BLOCK
    vmask = (rows_head[:, None] == (np.arange(A_WIDTH) // A_HEAD_DIM)[None, :]).astype(np.float32)
    return jnp.asarray(smask), jnp.asarray(vmask)


def _rope_tables():
    inv = ROPE_BASE ** (-jnp.arange(ROPE_HALF, dtype=F32) / ROPE_HALF)
    ang = jnp.arange(SEQ).astype(F32)[:, None] * inv[None, :]
    return jnp.tile(jnp.cos(ang), (1, HEAD_GROUP)), jnp.tile(jnp.sin(ang), (1, HEAD_GROUP))


def _mixer_kernel(x_ref, n1g_ref, win_ref, wcat_ref, smask_ref, abias_ref, avg_ref, vmask_ref,
                  wpool_ref, bscale_ref, cos_ref, sin_ref, d4_ref, qdec_ref, kdec_ref, cdec_ref,
                  maskk_ref, maskv_ref, masks_ref, mavg_ref, cng_ref, wout_ref, out_ref, state_ref,
                  halo_ref):
    tile = x_ref.shape[0]
    j = pl.program_id(1)

    @pl.when(j == 0)
    def _():
        state_ref[...] = jnp.zeros_like(state_ref)
        halo_ref[...] = jnp.zeros_like(halo_ref)

    x = x_ref[...]
    h = _rms(x, n1g_ref[...]).astype(BF16)
    proj = jnp.dot(h, win_ref[...], preferred_element_type=F32)

    za = jax.nn.gelu(proj[:, :2 * A_WIDTH])
    u = za[:, :A_WIDTH]
    v = _rms(za[:, A_WIDTH:], avg_ref[...])
    wc = (wcat_ref[...] * smask_ref[...]).astype(BF16)
    vmask = vmask_ref[...]
    abias = abias_ref[...]
    sv = []
    for n in range(tile // A_BLOCK):
        vb = v[n * A_BLOCK:(n + 1) * A_BLOCK]
        v4 = (jnp.concatenate([vb] * A_HEADS, axis=0) * vmask).astype(BF16)
        sv.append(jnp.dot(wc, v4, preferred_element_type=F32) + abias)
    ya = u * jnp.concatenate(sv, axis=0)

    xb = proj[:, 2 * A_WIDTH:2 * A_WIDTH + B_WIDTH]
    ext = jnp.concatenate([halo_ref[...], xb], axis=0)
    halo_ref[...] = xb[tile - POOL_HALO:]
    s2 = ext + pltpu.roll(ext, 1, 0)
    s4 = s2 + pltpu.roll(s2, 2, 0)
    s8 = s4 + pltpu.roll(s4, 4, 0)
    s16 = s8 + pltpu.roll(s8, 8, 0)
    grp = lax.broadcasted_iota(jnp.int32, (1, B_WIDTH), 1) // B_GROUP_DIM
    wsum = jnp.where(grp == 0, s2, jnp.where(grp == 1, s4, jnp.where(grp == 2, s8, s16)))[POOL_HALO:]
    win = jnp.where(grp == 0, 2, jnp.where(grp == 1, 4, jnp.where(grp == 2, 8, 16)))
    pos = j * tile + lax.broadcasted_iota(jnp.int32, (tile, 1), 0)
    cnt = jnp.minimum(pos + 1, win).astype(F32)
    pooled = wsum / cnt - xb
    yb = jnp.dot(pooled.astype(BF16), wpool_ref[...], preferred_element_type=F32) * bscale_ref[...]

    cosv = cos_ref[...]
    sinv = sin_ref[...]
    mask_k = maskk_ref[...]
    mask_v = maskv_ref[...]
    mask_s = masks_ref[...]
    mavg = mavg_ref[...]
    half = GROUP_LANES // 2

    def rope(a):
        a1, a2 = a[:, :half], a[:, half:]
        return jnp.concatenate([a1 * cosv - a2 * sinv, a2 * cosv + a1 * sinv], axis=1)

    yc = []
    for gi in range(N_GROUPS):
        base = 2 * A_WIDTH + B_WIDTH + gi * 4 * GROUP_LANES
        q = rope(proj[:, base:base + GROUP_LANES]) * (C_HEAD_DIM ** -0.5)
        k = rope(proj[:, base + GROUP_LANES:base + 2 * GROUP_LANES])
        vv = proj[:, base + 2 * GROUP_LANES:base + 3 * GROUP_LANES]
        gate = proj[:, base + 3 * GROUP_LANES:base + 4 * GROUP_LANES]
        d4 = d4_ref[gi]
        qdec = qdec_ref[gi]
        kdec = kdec_ref[gi]
        cdec = cdec_ref[gi]
        state = state_ref[gi]
        ys = []
        for c in range(tile // CHUNK):
            rows = slice(c * CHUNK, (c + 1) * CHUNK)
            qc, kc, vc = q[rows], k[rows], vv[rows]
            k4 = (jnp.concatenate([kc] * HEAD_GROUP, axis=0) * mask_k).astype(BF16)
            v4 = (jnp.concatenate([vc] * HEAD_GROUP, axis=0) * mask_v).astype(BF16)
            scores = lax.dot_general(qc.astype(BF16), k4, (((1,), (1,)), ((), ())),
                                     preferred_element_type=F32)
            y = jnp.dot((scores * d4).astype(BF16), v4, preferred_element_type=F32)
            y = y + jnp.dot((qc * qdec).astype(BF16), state.astype(BF16), preferred_element_type=F32)
            kv = lax.dot_general((kc * kdec).astype(BF16), vc.astype(BF16), (((0,), (0,)), ((), ())),
                                 preferred_element_type=F32)
            state = state * cdec + kv * mask_s
            ys.append(y)
        state_ref[gi] = state
        y4 = jnp.concatenate(ys, axis=0)
        y_hi = y4.astype(BF16)
        y_lo = (y4 - y_hi.astype(F32)).astype(BF16)
        mu = jnp.dot(y_hi, mavg, preferred_element_type=F32) + jnp.dot(y_lo, mavg, preferred_element_type=F32)
        dev = y4 - mu
        var = jnp.dot((dev * dev).astype(BF16), mavg, preferred_element_type=F32)
        yn = dev * lax.rsqrt(var + EPS) * cng_ref[:, gi * GROUP_LANES:(gi + 1) * GROUP_LANES]
        yc.append(jax.nn.silu(gate) * yn)

    mix = jnp.concatenate([ya, yb] + yc, axis=1).astype(BF16)
    out_ref[...] = x + jnp.dot(mix, wout_ref[...], preferred_element_type=F32)


def _ffn_kernel(x_ref, n2g_ref, wup_ref, convw_ref, convb_ref, wdown_ref, fg_ref, out_ref, halo_ref,
                *, apply_final_norm):
    tile = x_ref.shape[0]
    j = pl.program_id(1)

    @pl.when(j == 0)
    def _():
        halo_ref[...] = jnp.zeros_like(halo_ref)

    x = x_ref[...]
    h = _rms(x, n2g_ref[...]).astype(BF16)
    acc = x
    for c in range(D_FF // FF_CHUNK):
        conv = []
        for part in range(2):
            cols = slice(part * D_FF + c * FF_CHUNK, part * D_FF + (c + 1) * FF_CHUNK)
            up = jnp.dot(h, wup_ref[:, cols], preferred_element_type=F32)
            ext = jnp.concatenate([halo_ref[:, cols], up], axis=0)
            halo_ref[:, cols] = up[tile - CONV_HALO:]
            up1 = pltpu.roll(ext, 1, 0)[CONV_HALO:]
            up2 = pltpu.roll(ext, 2, 0)[CONV_HALO:]
            w = convw_ref[:, cols]
            conv.append(convb_ref[:, cols] + up2 * w[0:1] + up1 * w[1:2] + up * w[2:3])
        act = (jax.nn.silu(conv[0]) * conv[1]).astype(BF16)
        acc = acc + jnp.dot(act, wdown_ref[c * FF_CHUNK:(c + 1) * FF_CHUNK, :], preferred_element_type=F32)
    if apply_final_norm:
        acc = _rms(acc, fg_ref[...])
    out_ref[...] = acc


def _const_spec(shape):
    zeros = (0,) * len(shape)
    return pl.BlockSpec(shape, lambda b, j: zeros)


def _compiler_params():
    return pltpu.CompilerParams(dimension_semantics=("arbitrary", "arbitrary"),
                                vmem_limit_bytes=VMEM_LIMIT_BYTES)


def _mixer_call(x, consts, weights):
    batch, seq, _ = x.shape
    tile = TILE_MIX
    tile_spec = pl.BlockSpec((None, tile, D_MODEL), lambda b, j: (b, j, 0))
    rope_spec = pl.BlockSpec((tile, GROUP_LANES // 2), lambda b, j: (j, 0))
    (n1g, win, wcat, abias, avg, wpool, bscale, cng, wout) = weights
    (smask, vmask, cos_t, sin_t, d4, qdec, kdec, cdec, mask_k, mask_v, mask_s, mavg) = consts
    operands = [x, n1g, win, wcat, smask, abias, avg, vmask, wpool, bscale, cos_t, sin_t,
                d4, qdec, kdec, cdec, mask_k, mask_v, mask_s, mavg, cng, wout]
    in_specs = [tile_spec] + [_const_spec(a.shape) for a in operands[1:]]
    in_specs[10] = rope_spec
    in_specs[11] = rope_spec
    return pl.pallas_call(
        _mixer_kernel,
        grid=(batch, seq // tile),
        in_specs=in_specs,
        out_specs=tile_spec,
        out_shape=jax.ShapeDtypeStruct(x.shape, x.dtype),
        scratch_shapes=[pltpu.VMEM((N_GROUPS, GROUP_LANES, GROUP_LANES), F32),
                        pltpu.VMEM((POOL_HALO, B_WIDTH), F32)],
        compiler_params=_compiler_params(),
        name="mixer",
    )(*operands)


def _ffn_call(x, weights, apply_final_norm):
    batch, seq, _ = x.shape
    tile = TILE_FFN
    tile_spec = pl.BlockSpec((None, tile, D_MODEL), lambda b, j: (b, j, 0))
    in_specs = [tile_spec] + [_const_spec(a.shape) for a in weights]
    return pl.pallas_call(
        functools.partial(_ffn_kernel, apply_final_norm=apply_final_norm),
        grid=(batch, seq // tile),
        in_specs=in_specs,
        out_specs=tile_spec,
        out_shape=jax.ShapeDtypeStruct(x.shape, x.dtype),
        scratch_shapes=[pltpu.VMEM((CONV_HALO, 2 * D_FF), F32)],
        compiler_params=_compiler_params(),
        name="ffn",
    )(x, *weights)


def kernel(x, norm1_g, w_in, a_vnorm_g, a_ws, a_bs, b_w, b_scale, c_norm_g, w_out, norm2_g, w_up,
           conv_w, conv_b, w_down, final_g):
    depth = w_in.shape[0]
    assert x.shape[1] == SEQ and x.shape[2] == D_MODEL
    col_order = _in_proj_column_order()
    smask, vmask = _spatial_constants()
    cos_t, sin_t = _rope_tables()
    d4, qdec, kdec, cdec, mask_k, mask_v = _retention_constants()
    mavg = (mask_v / C_HEAD_DIM).astype(BF16)
    consts = (smask, vmask, cos_t, sin_t, d4, qdec, kdec, cdec, mask_k, mask_v, mask_k.T, mavg)
    row = lambda a: a.reshape(1, -1)
    for l in range(depth):
        mixer_weights = (
            row(norm1_g[l]),
            w_in[l][:, col_order].astype(BF16),
            a_ws[l].transpose(1, 0, 2).reshape(A_BLOCK, A_HEADS * A_BLOCK),
            jnp.repeat(a_bs[l].T, A_HEAD_DIM, axis=1),
            row(a_vnorm_g[l]),
            jax.scipy.linalg.block_diag(*[b_w[l, g] for g in range(len(POOL_WINDOWS))]).astype(BF16),
            row(b_scale[l]),
            row(c_norm_g[l]),
            w_out[l].astype(BF16),
        )
        x = _mixer_call(x, consts, mixer_weights)
        ffn_weights = (row(norm2_g[l]), w_up[l].astype(BF16), conv_w[l], row(conv_b[l]),
                       w_down[l].astype(BF16), row(final_g))
        x = _ffn_call(x, ffn_weights, apply_final_norm=(l == depth - 1))
    return x
```

```python
import functools

import numpy as np
import jax
import jax.numpy as jnp
from jax import lax
from jax.experimental import pallas as pl
from jax.experimental.pallas import tpu as pltpu

D_MODEL = 1024
SEQ = 8192
CHUNK = 64
A_WIDTH = 256
A_HEADS = 4
A_HEAD_DIM = A_WIDTH // A_HEADS
A_BLOCK = 128
POOL_WINDOWS = (2, 4, 8, 16)
B_WIDTH = 256
B_GROUP_DIM = B_WIDTH // len(POOL_WINDOWS)
C_WIDTH = 512
C_HEADS = 8
C_HEAD_DIM = C_WIDTH // C_HEADS
ROPE_HALF = C_HEAD_DIM // 2
ROPE_BASE = 10000.0
IN_COLS = 2 * A_WIDTH + B_WIDTH + 4 * C_WIDTH
D_FF = 2816
CONV_WIDTH = 3
EPS = 1e-6

HEAD_GROUP = 4
GROUP_LANES = HEAD_GROUP * C_HEAD_DIM
N_GROUPS = C_HEADS // HEAD_GROUP
POOL_HALO = 16
CONV_HALO = 8

TILE_MIX = 512
TILE_FFN = 512
FF_CHUNK = 256
VMEM_LIMIT_BYTES = 56 * 1024 * 1024
SINGLE_BUFFER_MIN_ELEMS = 1 << 20

F32 = jnp.float32
BF16 = jnp.bfloat16


def _rms(x, g):
    return x * lax.rsqrt(jnp.mean(x * x, axis=-1, keepdims=True) + EPS) * g


def _qk_lane_head(lane):
    return (lane % (GROUP_LANES // 2)) // ROPE_HALF


def _in_proj_column_order():
    cols = list(range(2 * A_WIDTH + B_WIDTH))
    base = 2 * A_WIDTH + B_WIDTH
    for gi in range(N_GROUPS):
        for part in range(4):
            start = base + part * C_WIDTH + gi * GROUP_LANES
            if part < 2:
                for lane in range(GROUP_LANES):
                    head = _qk_lane_head(lane)
                    second = lane // (GROUP_LANES // 2)
                    cols.append(start + head * C_HEAD_DIM + second * ROPE_HALF + lane % ROPE_HALF)
            else:
                cols.extend(range(start, start + GROUP_LANES))
    return np.asarray(cols, np.int32)


def _retention_constants():
    log_gamma = np.log1p(-np.exp2(-5.0 - np.arange(C_HEADS, dtype=np.float64)))
    idx = np.arange(CHUNK, dtype=np.float64)
    lane = np.arange(GROUP_LANES)
    d4 = np.zeros((N_GROUPS, CHUNK, GROUP_LANES))
    qdec = np.zeros((N_GROUPS, CHUNK, GROUP_LANES))
    kdec = np.zeros((N_GROUPS, CHUNK, GROUP_LANES))
    cdec = np.zeros((N_GROUPS, 1, GROUP_LANES))
    for gi in range(N_GROUPS):
        lg_v = log_gamma[gi * HEAD_GROUP + lane // C_HEAD_DIM]
        lg_qk = log_gamma[gi * HEAD_GROUP + _qk_lane_head(lane)]
        d4[gi] = np.exp(lg_v[None, :] * np.abs(idx[:, None] - (lane % CHUNK)[None, :]))
        qdec[gi] = np.exp(lg_qk[None, :] * (idx + 1)[:, None])
        kdec[gi] = np.exp(lg_qk[None, :] * (CHUNK - 1 - idx)[:, None])
        cdec[gi] = np.exp(lg_v * CHUNK)[None, :]
    rows_head = np.arange(GROUP_LANES) // CHUNK
    mask_k = (rows_head[:, None] == _qk_lane_head(lane)[None, :]).astype(np.float32)
    mask_v = (rows_head[:, None] == (lane // C_HEAD_DIM)[None, :]).astype(np.float32)
    f = lambda a: jnp.asarray(a, F32)
    return f(d4), f(qdec), f(kdec), f(cdec), f(mask_k), f(mask_v)


def _spatial_constants():
    chunk_id = np.arange(A_BLOCK) // CHUNK
    causal = (chunk_id[:, None] >= chunk_id[None, :]).astype(np.float32)
    smask = np.tile(causal, (1, A_HEADS))
    rows_head = np.arange(A_HEADS * A_BLOCK) // A_BLOCK
    vmask = (rows_head[:, None] == (np.arange(A_WIDTH) // A_HEAD_DIM)[None, :]).astype(np.float32)
    return jnp.asarray(smask), jnp.asarray(vmask)


def _rope_tables():
    inv = ROPE_BASE ** (-jnp.arange(ROPE_HALF, dtype=F32) / ROPE_HALF)
    ang = jnp.arange(SEQ).astype(F32)[:, None] * inv[None, :]
    return jnp.tile(jnp.cos(ang), (1, HEAD_GROUP)), jnp.tile(jnp.sin(ang), (1, HEAD_GROUP))


def _mixer_kernel(x_ref, n1g_ref, win_ref, wcat_ref, smask_ref, abias_ref, avg_ref, vmask_ref,
                  wpool_ref, bscale_ref, cos_ref, sin_ref, d4_ref, qdec_ref, kdec_ref, cdec_ref,
                  maskk_ref, maskv_ref, masks_ref, mavg_ref, cng_ref, wout_ref, out_ref, state_ref,
                  halo_ref):
    tile = x_ref.shape[0]
    j = pl.program_id(1)

    @pl.when(j == 0)
    def _():
        state_ref[...] = jnp.zeros_like(state_ref)
        halo_ref[...] = jnp.zeros_like(halo_ref)

    x = x_ref[...]
    h = _rms(x, n1g_ref[...]).astype(BF16)
    proj = jnp.dot(h, win_ref[...], preferred_element_type=F32)

    za = jax.nn.gelu(proj[:, :2 * A_WIDTH])
    u = za[:, :A_WIDTH]
    v = _rms(za[:, A_WIDTH:], avg_ref[...])
    wc = (wcat_ref[...] * smask_ref[...]).astype(BF16)
    vmask = vmask_ref[...]
    abias = abias_ref[...]
    sv = []
    for n in range(tile // A_BLOCK):
        vb = v[n * A_BLOCK:(n + 1) * A_BLOCK]
        v4 = (jnp.concatenate([vb] * A_HEADS, axis=0) * vmask).astype(BF16)
        sv.append(jnp.dot(wc, v4, preferred_element_type=F32) + abias)
    ya = u * jnp.concatenate(sv, axis=0)

    xb = proj[:, 2 * A_WIDTH:2 * A_WIDTH + B_WIDTH]
    ext = jnp.concatenate([halo_ref[...], xb], axis=0)
    halo_ref[...] = xb[tile - POOL_HALO:]
    s2 = ext + pltpu.roll(ext, 1, 0)
    s4 = s2 + pltpu.roll(s2, 2, 0)
    s8 = s4 + pltpu.roll(s4, 4, 0)
    s16 = s8 + pltpu.roll(s8, 8, 0)
    grp = lax.broadcasted_iota(jnp.int32, (1, B_WIDTH), 1) // B_GROUP_DIM
    wsum = jnp.where(grp == 0, s2, jnp.where(grp == 1, s4, jnp.where(grp == 2, s8, s16)))[POOL_HALO:]
    win = jnp.where(grp == 0, 2, jnp.where(grp == 1, 4, jnp.where(grp == 2, 8, 16)))
    pos = j * tile + lax.broadcasted_iota(jnp.int32, (tile, 1), 0)
    cnt = jnp.minimum(pos + 1, win).astype(F32)
    pooled = wsum / cnt - xb
    yb = jnp.dot(pooled.astype(BF16), wpool_ref[...], preferred_element_type=F32) * bscale_ref[...]

    cosv = cos_ref[...]
    sinv = sin_ref[...]
    mask_k = maskk_ref[...]
    mask_v = maskv_ref[...]
    mask_s = masks_ref[...]
    mavg = mavg_ref[...]
    half = GROUP_LANES // 2

    def rope(a):
        a1, a2 = a[:, :half], a[:, half:]
        return jnp.concatenate([a1 * cosv - a2 * sinv, a2 * cosv + a1 * sinv], axis=1)

    yc = []
    for gi in range(N_GROUPS):
        base = 2 * A_WIDTH + B_WIDTH + gi * 4 * GROUP_LANES
        q = rope(proj[:, base:base + GROUP_LANES]) * (C_HEAD_DIM ** -0.5)
        k = rope(proj[:, base + GROUP_LANES:base + 2 * GROUP_LANES])
        vv = proj[:, base + 2 * GROUP_LANES:base + 3 * GROUP_LANES]
        gate = proj[:, base + 3 * GROUP_LANES:base + 4 * GROUP_LANES]
        d4 = d4_ref[gi]
        qdec = qdec_ref[gi]
        kdec = kdec_ref[gi]
        cdec = cdec_ref[gi]
        state = state_ref[gi]
        ys = []
        for c in range(tile // CHUNK):
            rows = slice(c * CHUNK, (c + 1) * CHUNK)
            qc, kc, vc = q[rows], k[rows], vv[rows]
            k4 = (jnp.concatenate([kc] * HEAD_GROUP, axis=0) * mask_k).astype(BF16)
            v4 = (jnp.concatenate([vc] * HEAD_GROUP, axis=0) * mask_v).astype(BF16)
            scores = lax.dot_general(qc.astype(BF16), k4, (((1,), (1,)), ((), ())),
                                     preferred_element_type=F32)
            y = jnp.dot((scores * d4).astype(BF16), v4, preferred_element_type=F32)
            y = y + jnp.dot((qc * qdec).astype(BF16), state.astype(BF16), preferred_element_type=F32)
            kv = lax.dot_general((kc * kdec).astype(BF16), vc.astype(BF16), (((0,), (0,)), ((), ())),
                                 preferred_element_type=F32)
            state = state * cdec + kv * mask_s
            ys.append(y)
        state_ref[gi] = state
        y4 = jnp.concatenate(ys, axis=0)
        y_hi = y4.astype(BF16)
        y_lo = (y4 - y_hi.astype(F32)).astype(BF16)
        mu = jnp.dot(y_hi, mavg, preferred_element_type=F32) + jnp.dot(y_lo, mavg, preferred_element_type=F32)
        dev = y4 - mu
        var = jnp.dot((dev * dev).astype(BF16), mavg, preferred_element_type=F32)
        yn = dev * lax.rsqrt(var + EPS) * cng_ref[:, gi * GROUP_LANES:(gi + 1) * GROUP_LANES]
        yc.append(jax.nn.silu(gate) * yn)

    mix = jnp.concatenate([ya, yb] + yc, axis=1).astype(BF16)
    out_ref[...] = x + jnp.dot(mix, wout_ref[...], preferred_element_type=F32)


def _ffn_kernel(x_ref, n2g_ref, wup_ref, convw_ref, convb_ref, wdown_ref, fg_ref, out_ref, halo_ref,
                *, apply_final_norm):
    tile = x_ref.shape[0]
    j = pl.program_id(1)

    @pl.when(j == 0)
    def _():
        halo_ref[...] = jnp.zeros_like(halo_ref)

    x = x_ref[...]
    h = _rms(x, n2g_ref[...]).astype(BF16)

    def up_proj(c):
        return [jnp.dot(h, wup_ref[:, part * D_FF + c * FF_CHUNK:part * D_FF + (c + 1) * FF_CHUNK],
                        preferred_element_type=F32) for part in range(2)]

    n_chunks = D_FF // FF_CHUNK
    acts = []
    nxt = up_proj(0)
    for c in range(n_chunks):
        cur = nxt
        if c + 1 < n_chunks:
            nxt = up_proj(c + 1)
        conv = []
        for part in range(2):
            cols = slice(part * D_FF + c * FF_CHUNK, part * D_FF + (c + 1) * FF_CHUNK)
            up = cur[part]
            ext = jnp.concatenate([halo_ref[:, cols], up], axis=0)
            halo_ref[:, cols] = up[tile - CONV_HALO:]
            up1 = pltpu.roll(ext, 1, 0)[CONV_HALO:]
            up2 = pltpu.roll(ext, 2, 0)[CONV_HALO:]
            w = convw_ref[:, cols]
            conv.append(convb_ref[:, cols] + up2 * w[0:1] + up1 * w[1:2] + up * w[2:3])
        acts.append((jax.nn.silu(conv[0]) * conv[1]).astype(BF16))
    act = jnp.concatenate(acts, axis=1)
    acc = x + jnp.dot(act, wdown_ref[...], preferred_element_type=F32)
    if apply_final_norm:
        acc = _rms(acc, fg_ref[...])
    out_ref[...] = acc


def _const_spec(shape):
    zeros = (0,) * len(shape)
    if int(np.prod(shape)) >= SINGLE_BUFFER_MIN_ELEMS:
        return pl.BlockSpec(shape, lambda b, j: zeros, pipeline_mode=pl.Buffered(1))
    return pl.BlockSpec(shape, lambda b, j: zeros)


def _compiler_params():
    return pltpu.CompilerParams(dimension_semantics=("arbitrary", "arbitrary"),
                                vmem_limit_bytes=VMEM_LIMIT_BYTES)


def _mixer_call(x, consts, weights):
    batch, seq, _ = x.shape
    tile = TILE_MIX
    tile_spec = pl.BlockSpec((None, tile, D_MODEL), lambda b, j: (b, j, 0))
    rope_spec = pl.BlockSpec((tile, GROUP_LANES // 2), lambda b, j: (j, 0))
    (n1g, win, wcat, abias, avg, wpool, bscale, cng, wout) = weights
    (smask, vmask, cos_t, sin_t, d4, qdec, kdec, cdec, mask_k, mask_v, mask_s, mavg) = consts
    operands = [x, n1g, win, wcat, smask, abias, avg, vmask, wpool, bscale, cos_t, sin_t,
                d4, qdec, kdec, cdec, mask_k, mask_v, mask_s, mavg, cng, wout]
    in_specs = [tile_spec] + [_const_spec(a.shape) for a in operands[1:]]
    in_specs[10] = rope_spec
    in_specs[11] = rope_spec
    return pl.pallas_call(
        _mixer_kernel,
        grid=(batch, seq // tile),
        in_specs=in_specs,
        out_specs=tile_spec,
        out_shape=jax.ShapeDtypeStruct(x.shape, x.dtype),
        scratch_shapes=[pltpu.VMEM((N_GROUPS, GROUP_LANES, GROUP_LANES), F32),
                        pltpu.VMEM((POOL_HALO, B_WIDTH), F32)],
        compiler_params=_compiler_params(),
        name="mixer",
    )(*operands)


def _ffn_call(x, weights, apply_final_norm):
    batch, seq, _ = x.shape
    tile = TILE_FFN
    tile_spec = pl.BlockSpec((None, tile, D_MODEL), lambda b, j: (b, j, 0))
    in_specs = [tile_spec] + [_const_spec(a.shape) for a in weights]
    return pl.pallas_call(
        functools.partial(_ffn_kernel, apply_final_norm=apply_final_norm),
        grid=(batch, seq // tile),
        in_specs=in_specs,
        out_specs=tile_spec,
        out_shape=jax.ShapeDtypeStruct(x.shape, x.dtype),
        scratch_shapes=[pltpu.VMEM((CONV_HALO, 2 * D_FF), F32)],
        compiler_params=_compiler_params(),
        name="ffn",
    )(x, *weights)


def kernel(x, norm1_g, w_in, a_vnorm_g, a_ws, a_bs, b_w, b_scale, c_norm_g, w_out, norm2_g, w_up,
           conv_w, conv_b, w_down, final_g):
    depth = w_in.shape[0]
    assert x.shape[1] == SEQ and x.shape[2] == D_MODEL
    col_order = _in_proj_column_order()
    smask, vmask = _spatial_constants()
    cos_t, sin_t = _rope_tables()
    d4, qdec, kdec, cdec, mask_k, mask_v = _retention_constants()
    mavg = (mask_v / C_HEAD_DIM).astype(BF16)
    consts = (smask, vmask, cos_t, sin_t, d4, qdec, kdec, cdec, mask_k, mask_v, mask_k.T, mavg)
    row = lambda a: a.reshape(1, -1)
    for l in range(depth):
        mixer_weights = (
            row(norm1_g[l]),
            w_in[l][:, col_order].astype(BF16),
            a_ws[l].transpose(1, 0, 2).reshape(A_BLOCK, A_HEADS * A_BLOCK),
            jnp.repeat(a_bs[l].T, A_HEAD_DIM, axis=1),
            row(a_vnorm_g[l]),
            jax.scipy.linalg.block_diag(*[b_w[l, g] for g in range(len(POOL_WINDOWS))]).astype(BF16),
            row(b_scale[l]),
            row(c_norm_g[l]),
            w_out[l].astype(BF16),
        )
        x = _mixer_call(x, consts, mixer_weights)
        ffn_weights = (row(norm2_g[l]), w_up[l].astype(BF16), conv_w[l], row(conv_b[l]),
                       w_down[l].astype(BF16), row(final_g))
        x = _ffn_call(x, ffn_weights, apply_final_norm=(l == depth - 1))
    return x
```

```python
import functools

import numpy as np
import jax
import jax.numpy as jnp
from jax import lax
from jax.experimental import pallas as pl
from jax.experimental.pallas import tpu as pltpu

D_MODEL = 1024
SEQ = 8192
CHUNK = 64
A_WIDTH = 256
A_HEADS = 4
A_HEAD_DIM = A_WIDTH // A_HEADS
A_BLOCK = 128
POOL_WINDOWS = (2, 4, 8, 16)
B_WIDTH = 256
B_GROUP_DIM = B_WIDTH // len(POOL_WINDOWS)
C_WIDTH = 512
C_HEADS = 8
C_HEAD_DIM = C_WIDTH // C_HEADS
ROPE_HALF = C_HEAD_DIM // 2
ROPE_BASE = 10000.0
IN_COLS = 2 * A_WIDTH + B_WIDTH + 4 * C_WIDTH
D_FF = 2816
CONV_WIDTH = 3
EPS = 1e-6

HEAD_GROUP = 4
GROUP_LANES = HEAD_GROUP * C_HEAD_DIM
N_GROUPS = C_HEADS // HEAD_GROUP
POOL_HALO = 16
CONV_HALO = 8

TILE_MIX = 512
TILE_FFN = 1024
FF_CHUNK = 256
VMEM_LIMIT_BYTES = 56 * 1024 * 1024
SINGLE_BUFFER_MIN_ELEMS = 1 << 20

F32 = jnp.float32
BF16 = jnp.bfloat16


def _rms(x, g):
    return x * lax.rsqrt(jnp.mean(x * x, axis=-1, keepdims=True) + EPS) * g


def _qk_lane_head(lane):
    return (lane % (GROUP_LANES // 2)) // ROPE_HALF


def _reorder_in_proj(w_in):
    n_layers = w_in.shape[0]
    front = w_in[:, :, :2 * A_WIDTH + B_WIDTH]
    rest = w_in[:, :, 2 * A_WIDTH + B_WIDTH:].reshape(
        n_layers, D_MODEL, 4, N_GROUPS, HEAD_GROUP, 2, ROPE_HALF)
    qk = rest[:, :, :2].transpose(0, 1, 3, 2, 5, 4, 6)
    vg = rest[:, :, 2:].transpose(0, 1, 3, 2, 4, 5, 6)
    qk = qk.reshape(n_layers, D_MODEL, N_GROUPS, 2 * GROUP_LANES)
    vg = vg.reshape(n_layers, D_MODEL, N_GROUPS, 2 * GROUP_LANES)
    rest = jnp.concatenate([qk, vg], axis=3).reshape(n_layers, D_MODEL, 4 * C_WIDTH)
    return jnp.concatenate([front, rest], axis=2).astype(BF16)


def _retention_constants():
    log_gamma = np.log1p(-np.exp2(-5.0 - np.arange(C_HEADS, dtype=np.float64)))
    idx = np.arange(CHUNK, dtype=np.float64)
    lane = np.arange(GROUP_LANES)
    d4 = np.zeros((N_GROUPS, CHUNK, GROUP_LANES))
    qdec = np.zeros((N_GROUPS, CHUNK, GROUP_LANES))
    kdec = np.zeros((N_GROUPS, CHUNK, GROUP_LANES))
    cdec = np.zeros((N_GROUPS, 1, GROUP_LANES))
    for gi in range(N_GROUPS):
        lg_v = log_gamma[gi * HEAD_GROUP + lane // C_HEAD_DIM]
        lg_qk = log_gamma[gi * HEAD_GROUP + _qk_lane_head(lane)]
        d4[gi] = np.exp(lg_v[None, :] * np.abs(idx[:, None] - (lane % CHUNK)[None, :]))
        qdec[gi] = np.exp(lg_qk[None, :] * (idx + 1)[:, None])
        kdec[gi] = np.exp(lg_qk[None, :] * (CHUNK - 1 - idx)[:, None])
        cdec[gi] = np.exp(lg_v * CHUNK)[None, :]
    rows_head = np.arange(GROUP_LANES) // CHUNK
    mask_k = (rows_head[:, None] == _qk_lane_head(lane)[None, :]).astype(np.float32)
    mask_v = (rows_head[:, None] == (lane // C_HEAD_DIM)[None, :]).astype(np.float32)
    f = lambda a: jnp.asarray(a, F32)
    return (f(d4), f(qdec), f(kdec), f(cdec), jnp.asarray(mask_k, BF16), jnp.asarray(mask_v, BF16),
            f(mask_k.T), jnp.asarray(mask_v / C_HEAD_DIM, BF16))


def _spatial_constants():
    chunk_id = np.arange(A_BLOCK) // CHUNK
    causal = (chunk_id[:, None] >= chunk_id[None, :]).astype(np.float32)
    smask = np.tile(causal, (1, A_HEADS))
    rows_head = np.arange(A_HEADS * A_BLOCK) // A_BLOCK
    vmask = (rows_head[:, None] == (np.arange(A_WIDTH) // A_HEAD_DIM)[None, :]).astype(np.float32)
    return jnp.asarray(smask), jnp.asarray(vmask, BF16)


def _rope_tables():
    inv = ROPE_BASE ** (-jnp.arange(ROPE_HALF, dtype=F32) / ROPE_HALF)
    ang = jnp.arange(SEQ).astype(F32)[:, None] * inv[None, :]
    return jnp.tile(jnp.cos(ang), (1, HEAD_GROUP)), jnp.tile(jnp.sin(ang), (1, HEAD_GROUP))


def _mixer_kernel(x_ref, n1g_ref, win_ref, wcat_ref, smask_ref, abias_ref, avg_ref, vmask_ref,
                  wpool_ref, bscale_ref, cos_ref, sin_ref, d4_ref, qdec_ref, kdec_ref, cdec_ref,
                  maskk_ref, maskv_ref, masks_ref, mavg_ref, cng_ref, wout_ref, out_ref, state_ref,
                  halo_ref):
    tile = x_ref.shape[0]
    j = pl.program_id(1)

    @pl.when(j == 0)
    def _():
        state_ref[...] = jnp.zeros_like(state_ref)
        halo_ref[...] = jnp.zeros_like(halo_ref)

    x = x_ref[...]
    h = _rms(x, n1g_ref[...]).astype(BF16)
    proj = jnp.dot(h, win_ref[...], preferred_element_type=F32)
    n_chunks = tile // CHUNK
    half = GROUP_LANES // 2
    c_base = 2 * A_WIDTH + B_WIDTH

    za = jax.nn.gelu(proj[:, :2 * A_WIDTH])
    u = za[:, :A_WIDTH]
    v = _rms(za[:, A_WIDTH:], avg_ref[...])
    wc = (wcat_ref[...] * smask_ref[...]).astype(BF16)
    vmask = vmask_ref[...]
    abias = abias_ref[...]
    sv = []
    for n in range(tile // A_BLOCK):
        vb = v[n * A_BLOCK:(n + 1) * A_BLOCK].astype(BF16)
        v4 = jnp.concatenate([vb] * A_HEADS, axis=0) * vmask
        sv.append(jnp.dot(wc, v4, preferred_element_type=F32) + abias)
    ya = u * jnp.concatenate(sv, axis=0)

    xb = proj[:, 2 * A_WIDTH:c_base]
    ext = jnp.concatenate([halo_ref[...], xb], axis=0)
    halo_ref[...] = xb[tile - POOL_HALO:]
    s2 = ext + pltpu.roll(ext, 1, 0)
    s4 = s2 + pltpu.roll(s2, 2, 0)
    s8 = s4 + pltpu.roll(s4, 4, 0)
    s16 = s8 + pltpu.roll(s8, 8, 0)
    grp = lax.broadcasted_iota(jnp.int32, (1, B_WIDTH), 1) // B_GROUP_DIM
    wsum = jnp.where(grp == 0, s2, jnp.where(grp == 1, s4, jnp.where(grp == 2, s8, s16)))[POOL_HALO:]
    win = jnp.where(grp == 0, 2, jnp.where(grp == 1, 4, jnp.where(grp == 2, 8, 16)))
    pos = j * tile + lax.broadcasted_iota(jnp.int32, (tile, 1), 0)
    cnt = jnp.minimum(pos + 1, win).astype(F32)
    pooled = wsum / cnt - xb
    yb = jnp.dot(pooled.astype(BF16), wpool_ref[...], preferred_element_type=F32) * bscale_ref[...]

    cosv = cos_ref[...]
    sinv = sin_ref[...]
    mask_k = maskk_ref[...]
    mask_v = maskv_ref[...]
    mask_s = masks_ref[...]
    mavg = mavg_ref[...]

    def rope(a):
        a1, a2 = a[:, :half], a[:, half:]
        return jnp.concatenate([a1 * cosv - a2 * sinv, a2 * cosv + a1 * sinv], axis=1)

    groups = []
    for gi in range(N_GROUPS):
        base = c_base + gi * 4 * GROUP_LANES
        groups.append(dict(
            q=rope(proj[:, base:base + GROUP_LANES]) * (C_HEAD_DIM ** -0.5),
            k=rope(proj[:, base + GROUP_LANES:base + 2 * GROUP_LANES]),
            v=proj[:, base + 2 * GROUP_LANES:base + 3 * GROUP_LANES],
            gate=proj[:, base + 3 * GROUP_LANES:base + 4 * GROUP_LANES],
            d4=d4_ref[gi], qdec=qdec_ref[gi], kdec=kdec_ref[gi], cdec=cdec_ref[gi],
            state=state_ref[gi], ys=[]))

    def prepare(g, c):
        rows = slice(c * CHUNK, (c + 1) * CHUNK)
        qc, kc, vc = g["q"][rows], g["k"][rows], g["v"][rows]
        vb = vc.astype(BF16)
        k4 = jnp.concatenate([kc.astype(BF16)] * HEAD_GROUP, axis=0) * mask_k
        v4 = jnp.concatenate([vb] * HEAD_GROUP, axis=0) * mask_v
        scores = lax.dot_general(qc.astype(BF16), k4, (((1,), (1,)), ((), ())),
                                 preferred_element_type=F32)
        return dict(scores=scores, v4=v4, qd=(qc * g["qdec"]).astype(BF16),
                    kd=(kc * g["kdec"]).astype(BF16), vb=vb)

    def state_contribution(ch):
        return lax.dot_general(ch["kd"], ch["vb"], (((0,), (0,)), ((), ())),
                               preferred_element_type=F32)

    cur = [prepare(g, 0) for g in groups]
    for ch in cur:
        ch["kv"] = state_contribution(ch)
    for c in range(n_chunks):
        nxt = [prepare(g, c + 1) for g in groups] if c + 1 < n_chunks else None
        for g, ch in zip(groups, cur):
            p = (ch["scores"] * g["d4"]).astype(BF16)
            y = jnp.dot(p, ch["v4"], preferred_element_type=F32)
            y = y + jnp.dot(ch["qd"], g["state"].astype(BF16), preferred_element_type=F32)
            g["ys"].append(y)
        if nxt is not None:
            for ch in nxt:
                ch["kv"] = state_contribution(ch)
        for g, ch in zip(groups, cur):
            g["state"] = g["state"] * g["cdec"] + ch["kv"] * mask_s
        cur = nxt
    for gi, g in enumerate(groups):
        state_ref[gi] = g["state"]

    y4 = [jnp.concatenate(g["ys"], axis=0) for g in groups]
    mu = []
    for y in y4:
        y_hi = y.astype(BF16)
        y_lo = (y - y_hi.astype(F32)).astype(BF16)
        mu.append(jnp.dot(y_hi, mavg, preferred_element_type=F32)
                  + jnp.dot(y_lo, mavg, preferred_element_type=F32))
    dev = [y - m for y, m in zip(y4, mu)]
    var = [jnp.dot((d * d).astype(BF16), mavg, preferred_element_type=F32) for d in dev]
    yc = []
    for gi, g in enumerate(groups):
        yn = dev[gi] * lax.rsqrt(var[gi] + EPS) * cng_ref[:, gi * GROUP_LANES:(gi + 1) * GROUP_LANES]
        yc.append(jax.nn.silu(g["gate"]) * yn)

    mix = jnp.concatenate([ya, yb] + yc, axis=1).astype(BF16)
    out_ref[...] = x + jnp.dot(mix, wout_ref[...], preferred_element_type=F32)


def _ffn_kernel(x_ref, n2g_ref, wup_ref, convw_ref, convb_ref, wdown_ref, fg_ref, out_ref, halo_ref,
                *, apply_final_norm):
    tile = x_ref.shape[0]
    j = pl.program_id(1)

    @pl.when(j == 0)
    def _():
        halo_ref[...] = jnp.zeros_like(halo_ref)

    x = x_ref[...]
    h = _rms(x, n2g_ref[...]).astype(BF16)

    def up_proj(c):
        return [jnp.dot(h, wup_ref[:, part * D_FF + c * FF_CHUNK:part * D_FF + (c + 1) * FF_CHUNK],
                        preferred_element_type=F32) for part in range(2)]

    n_chunks = D_FF // FF_CHUNK
    acts = []
    nxt = up_proj(0)
    for c in range(n_chunks):
        cur = nxt
        if c + 1 < n_chunks:
            nxt = up_proj(c + 1)
        conv = []
        for part in range(2):
            cols = slice(part * D_FF + c * FF_CHUNK, part * D_FF + (c + 1) * FF_CHUNK)
            up = cur[part]
            ext = jnp.concatenate([halo_ref[:, cols], up], axis=0)
            halo_ref[:, cols] = up[tile - CONV_HALO:]
            up1 = pltpu.roll(ext, 1, 0)[CONV_HALO:]
            up2 = pltpu.roll(ext, 2, 0)[CONV_HALO:]
            w = convw_ref[:, cols]
            conv.append(convb_ref[:, cols] + up2 * w[0:1] + up1 * w[1:2] + up * w[2:3])
        acts.append((jax.nn.silu(conv[0]) * conv[1]).astype(BF16))
    act = jnp.concatenate(acts, axis=1)
    acc = x + jnp.dot(act, wdown_ref[...], preferred_element_type=F32)
    if apply_final_norm:
        acc = _rms(acc, fg_ref[...])
    out_ref[...] = acc


def _resident_spec(shape, layer=None):
    if layer is None:
        block, index = tuple(shape), (0,) * len(shape)
    else:
        block, index = (None,) + tuple(shape[1:]), (layer,) + (0,) * (len(shape) - 1)
    if int(np.prod(shape[(layer is not None):])) >= SINGLE_BUFFER_MIN_ELEMS:
        return pl.BlockSpec(block, lambda b, j: index, pipeline_mode=pl.Buffered(1))
    return pl.BlockSpec(block, lambda b, j: index)


def _compiler_params():
    return pltpu.CompilerParams(dimension_semantics=("arbitrary", "arbitrary"),
                                vmem_limit_bytes=VMEM_LIMIT_BYTES)


def _mixer_call(x, layer, consts, weights):
    batch, seq, _ = x.shape
    tile = TILE_MIX
    tile_spec = pl.BlockSpec((None, tile, D_MODEL), lambda b, j: (b, j, 0))
    rope_spec = pl.BlockSpec((tile, GROUP_LANES // 2), lambda b, j: (j, 0))
    (n1g, win, wcat, abias, avg, wpool, bscale, cng, wout) = weights
    (smask, vmask, cos_t, sin_t, d4, qdec, kdec, cdec, mask_k, mask_v, mask_s, mavg) = consts
    w = lambda a: (a, _resident_spec(a.shape, layer))
    k = lambda a: (a, _resident_spec(a.shape))
    operands = [(x, tile_spec), w(n1g), w(win), w(wcat), k(smask), w(abias), w(avg), k(vmask), w(wpool),
                w(bscale), (cos_t, rope_spec), (sin_t, rope_spec), k(d4), k(qdec), k(kdec), k(cdec),
                k(mask_k), k(mask_v), k(mask_s), k(mavg), w(cng), w(wout)]
    return pl.pallas_call(
        _mixer_kernel,
        grid=(batch, seq // tile),
        in_specs=[spec for _, spec in operands],
        out_specs=tile_spec,
        out_shape=jax.ShapeDtypeStruct(x.shape, x.dtype),
        scratch_shapes=[pltpu.VMEM((N_GROUPS, GROUP_LANES, GROUP_LANES), F32),
                        pltpu.VMEM((POOL_HALO, B_WIDTH), F32)],
        compiler_params=_compiler_params(),
        name="mixer",
    )(*[a for a, _ in operands])


def _ffn_call(x, layer, weights, final_g, apply_final_norm):
    batch, seq, _ = x.shape
    tile = TILE_FFN
    tile_spec = pl.BlockSpec((None, tile, D_MODEL), lambda b, j: (b, j, 0))
    in_specs = [tile_spec] + [_resident_spec(a.shape, layer) for a in weights] + [_resident_spec(final_g.shape)]
    return pl.pallas_call(
        functools.partial(_ffn_kernel, apply_final_norm=apply_final_norm),
        grid=(batch, seq // tile),
        in_specs=in_specs,
        out_specs=tile_spec,
        out_shape=jax.ShapeDtypeStruct(x.shape, x.dtype),
        scratch_shapes=[pltpu.VMEM((CONV_HALO, 2 * D_FF), F32)],
        compiler_params=_compiler_params(),
        name="ffn",
    )(x, *weights, final_g)


def kernel(x, norm1_g, w_in, a_vnorm_g, a_ws, a_bs, b_w, b_scale, c_norm_g, w_out, norm2_g, w_up,
           conv_w, conv_b, w_down, final_g):
    depth = w_in.shape[0]
    assert x.shape[1] == SEQ and x.shape[2] == D_MODEL
    smask, vmask = _spatial_constants()
    cos_t, sin_t = _rope_tables()
    d4, qdec, kdec, cdec, mask_k, mask_v, mask_s, mavg = _retention_constants()
    consts = (smask, vmask, cos_t, sin_t, d4, qdec, kdec, cdec, mask_k, mask_v, mask_s, mavg)
    rows = lambda a: a[:, None, :]
    n_pool = len(POOL_WINDOWS)
    pool_eye = jnp.eye(n_pool, dtype=F32)[None, :, None, :, None]
    mixer_weights = (
        rows(norm1_g),
        _reorder_in_proj(w_in),
        a_ws.transpose(0, 2, 1, 3).reshape(depth, A_BLOCK, A_HEADS * A_BLOCK),
        jnp.repeat(a_bs.transpose(0, 2, 1), A_HEAD_DIM, axis=2),
        rows(a_vnorm_g),
        (b_w[:, :, :, None, :] * pool_eye).reshape(depth, B_WIDTH, B_WIDTH).astype(BF16),
        rows(b_scale),
        rows(c_norm_g),
        w_out.astype(BF16),
    )
    ffn_weights = (rows(norm2_g), w_up.astype(BF16), conv_w, rows(conv_b), w_down.astype(BF16))
    final_row = final_g.reshape(1, -1)
    for l in range(depth):
        x = _mixer_call(x, l, consts, mixer_weights)
        x = _ffn_call(x, l, ffn_weights, final_row, apply_final_norm=(l == depth - 1))
    return x
```

```python
import functools

import numpy as np
import jax
import jax.numpy as jnp
from jax import lax
from jax.experimental import pallas as pl
from jax.experimental.pallas import tpu as pltpu

D_MODEL = 1024
SEQ = 8192
CHUNK = 64
A_WIDTH = 256
A_HEADS = 4
A_HEAD_DIM = A_WIDTH // A_HEADS
A_BLOCK = 128
POOL_WINDOWS = (2, 4, 8, 16)
B_WIDTH = 256
B_GROUP_DIM = B_WIDTH // len(POOL_WINDOWS)
C_WIDTH = 512
C_HEADS = 8
C_HEAD_DIM = C_WIDTH // C_HEADS
ROPE_HALF = C_HEAD_DIM // 2
ROPE_BASE = 10000.0
IN_COLS = 2 * A_WIDTH + B_WIDTH + 4 * C_WIDTH
D_FF = 2816
CONV_WIDTH = 3
EPS = 1e-6

HEAD_GROUP = 4
GROUP_LANES = HEAD_GROUP * C_HEAD_DIM
N_GROUPS = C_HEADS // HEAD_GROUP
POOL_HALO = 16
CONV_HALO = 8

TILE_MIX = 1024
TILE_FFN = 1024
FF_CHUNK = 256
VMEM_LIMIT_BYTES = 56 * 1024 * 1024
SINGLE_BUFFER_MIN_ELEMS = 1 << 20

F32 = jnp.float32
BF16 = jnp.bfloat16


def _rms(x, g):
    return x * lax.rsqrt(jnp.mean(x * x, axis=-1, keepdims=True) + EPS) * g


def _qk_lane_head(lane):
    return (lane % (GROUP_LANES // 2)) // ROPE_HALF


def _split_rotary_halves(w_qk):
    n_layers = w_qk.shape[0]
    w = w_qk.reshape(n_layers, D_MODEL, 2, N_GROUPS, HEAD_GROUP, 2, ROPE_HALF)
    return w.transpose(0, 1, 2, 3, 5, 4, 6).reshape(n_layers, D_MODEL, 2 * C_WIDTH)


def _retention_constants():
    log_gamma = np.log1p(-np.exp2(-5.0 - np.arange(C_HEADS, dtype=np.float64)))
    idx = np.arange(CHUNK, dtype=np.float64)
    lane = np.arange(GROUP_LANES)
    d4 = np.zeros((N_GROUPS, CHUNK, GROUP_LANES))
    qdec = np.zeros((N_GROUPS, CHUNK, GROUP_LANES))
    kdec = np.zeros((N_GROUPS, CHUNK, GROUP_LANES))
    cdec = np.zeros((N_GROUPS, 1, GROUP_LANES))
    for gi in range(N_GROUPS):
        lg_v = log_gamma[gi * HEAD_GROUP + lane // C_HEAD_DIM]
        lg_qk = log_gamma[gi * HEAD_GROUP + _qk_lane_head(lane)]
        d4[gi] = np.exp(lg_v[None, :] * np.abs(idx[:, None] - (lane % CHUNK)[None, :]))
        qdec[gi] = np.exp(lg_qk[None, :] * (idx + 1)[:, None])
        kdec[gi] = np.exp(lg_qk[None, :] * (CHUNK - 1 - idx)[:, None])
        cdec[gi] = np.exp(lg_v * CHUNK)[None, :]
    rows_head = np.arange(GROUP_LANES) // CHUNK
    mask_k = (rows_head[:, None] == _qk_lane_head(lane)[None, :]).astype(np.float32)
    mask_v = (rows_head[:, None] == (lane // C_HEAD_DIM)[None, :]).astype(np.float32)
    f = lambda a: jnp.asarray(a, F32)
    return (f(d4), f(qdec), f(kdec), f(cdec), jnp.asarray(mask_k, BF16), jnp.asarray(mask_v, BF16),
            f(mask_k.T), jnp.asarray(mask_v / C_HEAD_DIM, BF16))


def _spatial_constants():
    chunk_id = np.arange(A_BLOCK) // CHUNK
    causal = (chunk_id[:, None] >= chunk_id[None, :]).astype(np.float32)
    smask = np.tile(causal, (1, A_HEADS))
    rows_head = np.arange(A_HEADS * A_BLOCK) // A_BLOCK
    vmask = (rows_head[:, None] == (np.arange(A_WIDTH) // A_HEAD_DIM)[None, :]).astype(np.float32)
    return jnp.asarray(smask), jnp.asarray(vmask, BF16)


def _rope_tables():
    inv = ROPE_BASE ** (-np.arange(ROPE_HALF, dtype=np.float64) / ROPE_HALF)
    ang = np.arange(SEQ, dtype=np.float64)[:, None] * inv[None, :]
    tab = lambda a: jnp.asarray(np.tile(a, (1, HEAD_GROUP)), F32)
    return tab(np.cos(ang)), tab(np.sin(ang))


def _mixer_kernel(x_ref, n1g_ref, win_ref, wqk_ref, wcat_ref, smask_ref, abias_ref, avg_ref, vmask_ref,
                  wpool_ref, bscale_ref, cos_ref, sin_ref, d4_ref, qdec_ref, kdec_ref, cdec_ref,
                  maskk_ref, maskv_ref, masks_ref, mavg_ref, cng_ref, wout_ref, out_ref, state_ref,
                  halo_ref):
    tile = x_ref.shape[0]
    j = pl.program_id(1)

    @pl.when(j == 0)
    def _():
        state_ref[...] = jnp.zeros_like(state_ref)
        halo_ref[...] = jnp.zeros_like(halo_ref)

    x = x_ref[...]
    h = _rms(x, n1g_ref[...]).astype(BF16)
    ab_cols = 2 * A_WIDTH + B_WIDTH
    proj_ab = jnp.dot(h, win_ref[:, :ab_cols], preferred_element_type=F32)
    proj_qk = jnp.dot(h, wqk_ref[...], preferred_element_type=F32)
    proj_vg = jnp.dot(h, win_ref[:, ab_cols + 2 * C_WIDTH:], preferred_element_type=F32)
    n_chunks = tile // CHUNK
    half = GROUP_LANES // 2

    za = jax.nn.gelu(proj_ab[:, :2 * A_WIDTH])
    u = za[:, :A_WIDTH]
    v = _rms(za[:, A_WIDTH:], avg_ref[...])
    wc = (wcat_ref[...] * smask_ref[...]).astype(BF16)
    vmask = vmask_ref[...]
    abias = abias_ref[...]
    sv = []
    for n in range(tile // A_BLOCK):
        vb = v[n * A_BLOCK:(n + 1) * A_BLOCK].astype(BF16)
        v4 = jnp.concatenate([vb] * A_HEADS, axis=0) * vmask
        sv.append(jnp.dot(wc, v4, preferred_element_type=F32) + abias)
    ya = u * jnp.concatenate(sv, axis=0)

    xb = proj_ab[:, 2 * A_WIDTH:]
    ext = jnp.concatenate([halo_ref[...], xb], axis=0)
    halo_ref[...] = xb[tile - POOL_HALO:]
    s2 = ext + pltpu.roll(ext, 1, 0)
    s4 = s2 + pltpu.roll(s2, 2, 0)
    s8 = s4 + pltpu.roll(s4, 4, 0)
    s16 = s8 + pltpu.roll(s8, 8, 0)
    grp = lax.broadcasted_iota(jnp.int32, (1, B_WIDTH), 1) // B_GROUP_DIM
    wsum = jnp.where(grp == 0, s2, jnp.where(grp == 1, s4, jnp.where(grp == 2, s8, s16)))[POOL_HALO:]
    win = jnp.where(grp == 0, 2, jnp.where(grp == 1, 4, jnp.where(grp == 2, 8, 16)))
    pos = j * tile + lax.broadcasted_iota(jnp.int32, (tile, 1), 0)
    cnt = jnp.minimum(pos + 1, win).astype(F32)
    pooled = wsum / cnt - xb
    yb = jnp.dot(pooled.astype(BF16), wpool_ref[...], preferred_element_type=F32) * bscale_ref[...]

    cosv = cos_ref[...]
    sinv = sin_ref[...]
    mask_k = maskk_ref[...]
    mask_v = maskv_ref[...]
    mask_s = masks_ref[...]
    mavg = mavg_ref[...]

    def rope(a):
        a1, a2 = a[:, :half], a[:, half:]
        return jnp.concatenate([a1 * cosv - a2 * sinv, a2 * cosv + a1 * sinv], axis=1)

    groups = []
    for gi in range(N_GROUPS):
        lanes = slice(gi * GROUP_LANES, (gi + 1) * GROUP_LANES)
        k_lanes = slice(C_WIDTH + gi * GROUP_LANES, C_WIDTH + (gi + 1) * GROUP_LANES)
        groups.append(dict(
            q=rope(proj_qk[:, lanes]) * (C_HEAD_DIM ** -0.5),
            k=rope(proj_qk[:, k_lanes]),
            v=proj_vg[:, lanes],
            gate=proj_vg[:, k_lanes],
            d4=d4_ref[gi], qdec=qdec_ref[gi], kdec=kdec_ref[gi], cdec=cdec_ref[gi],
            state=state_ref[gi], ys=[]))

    def prepare(g, c):
        rows = slice(c * CHUNK, (c + 1) * CHUNK)
        qc, kc, vc = g["q"][rows], g["k"][rows], g["v"][rows]
        vb = vc.astype(BF16)
        k4 = jnp.concatenate([kc.astype(BF16)] * HEAD_GROUP, axis=0) * mask_k
        v4 = jnp.concatenate([vb] * HEAD_GROUP, axis=0) * mask_v
        scores = lax.dot_general(qc.astype(BF16), k4, (((1,), (1,)), ((), ())),
                                 preferred_element_type=F32)
        return dict(scores=scores, v4=v4, qd=(qc * g["qdec"]).astype(BF16),
                    kd=(kc * g["kdec"]).astype(BF16), vb=vb)

    def state_contribution(ch):
        return lax.dot_general(ch["kd"], ch["vb"], (((0,), (0,)), ((), ())),
                               preferred_element_type=F32)

    cur = [prepare(g, 0) for g in groups]
    for ch in cur:
        ch["kv"] = state_contribution(ch)
    for c in range(n_chunks):
        nxt = [prepare(g, c + 1) for g in groups] if c + 1 < n_chunks else None
        for g, ch in zip(groups, cur):
            p = (ch["scores"] * g["d4"]).astype(BF16)
            y = jnp.dot(p, ch["v4"], preferred_element_type=F32)
            y = y + jnp.dot(ch["qd"], g["state"].astype(BF16), preferred_element_type=F32)
            g["ys"].append(y)
        if nxt is not None:
            for ch in nxt:
                ch["kv"] = state_contribution(ch)
        for g, ch in zip(groups, cur):
            g["state"] = g["state"] * g["cdec"] + ch["kv"] * mask_s
        cur = nxt
    for gi, g in enumerate(groups):
        state_ref[gi] = g["state"]

    y4 = [jnp.concatenate(g["ys"], axis=0) for g in groups]
    mu = []
    for y in y4:
        y_hi = y.astype(BF16)
        y_lo = (y - y_hi.astype(F32)).astype(BF16)
        mu.append(jnp.dot(y_hi, mavg, preferred_element_type=F32)
                  + jnp.dot(y_lo, mavg, preferred_element_type=F32))
    dev = [y - m for y, m in zip(y4, mu)]
    var = [jnp.dot((d * d).astype(BF16), mavg, preferred_element_type=F32) for d in dev]
    yc = []
    for gi, g in enumerate(groups):
        yn = dev[gi] * lax.rsqrt(var[gi] + EPS) * cng_ref[:, gi * GROUP_LANES:(gi + 1) * GROUP_LANES]
        yc.append(jax.nn.silu(g["gate"]) * yn)

    mix = jnp.concatenate([ya, yb] + yc, axis=1).astype(BF16)
    out_ref[...] = x + jnp.dot(mix, wout_ref[...], preferred_element_type=F32)


def _ffn_kernel(x_ref, n2g_ref, wup_ref, convw_ref, convb_ref, wdown_ref, fg_ref, out_ref, halo_ref,
                *, apply_final_norm):
    tile = x_ref.shape[0]
    j = pl.program_id(1)

    @pl.when(j == 0)
    def _():
        halo_ref[...] = jnp.zeros_like(halo_ref)

    x = x_ref[...]
    h = _rms(x, n2g_ref[...]).astype(BF16)

    def up_proj(c):
        return [jnp.dot(h, wup_ref[:, part * D_FF + c * FF_CHUNK:part * D_FF + (c + 1) * FF_CHUNK],
                        preferred_element_type=F32) for part in range(2)]

    n_chunks = D_FF // FF_CHUNK
    acts = []
    nxt = up_proj(0)
    for c in range(n_chunks):
        cur = nxt
        if c + 1 < n_chunks:
            nxt = up_proj(c + 1)
        conv = []
        for part in range(2):
            cols = slice(part * D_FF + c * FF_CHUNK, part * D_FF + (c + 1) * FF_CHUNK)
            up = cur[part]
            ext = jnp.concatenate([halo_ref[:, cols], up], axis=0)
            halo_ref[:, cols] = up[tile - CONV_HALO:]
            up1 = pltpu.roll(ext, 1, 0)[CONV_HALO:]
            up2 = pltpu.roll(ext, 2, 0)[CONV_HALO:]
            w = convw_ref[:, cols]
            conv.append(convb_ref[:, cols] + up2 * w[0:1] + up1 * w[1:2] + up * w[2:3])
        acts.append((jax.nn.silu(conv[0]) * conv[1]).astype(BF16))
    act = jnp.concatenate(acts, axis=1)
    acc = x + jnp.dot(act, wdown_ref[...], preferred_element_type=F32)
    if apply_final_norm:
        acc = _rms(acc, fg_ref[...])
    out_ref[...] = acc


def _resident_spec(shape, layer=None):
    if layer is None:
        block, index = tuple(shape), (0,) * len(shape)
    else:
        block, index = (None,) + tuple(shape[1:]), (layer,) + (0,) * (len(shape) - 1)
    if int(np.prod(shape[(layer is not None):])) >= SINGLE_BUFFER_MIN_ELEMS:
        return pl.BlockSpec(block, lambda b, j: index, pipeline_mode=pl.Buffered(1))
    return pl.BlockSpec(block, lambda b, j: index)


def _compiler_params():
    return pltpu.CompilerParams(dimension_semantics=("arbitrary", "arbitrary"),
                                vmem_limit_bytes=VMEM_LIMIT_BYTES)


def _mixer_call(x, layer, consts, weights):
    batch, seq, _ = x.shape
    tile = TILE_MIX
    tile_spec = pl.BlockSpec((None, tile, D_MODEL), lambda b, j: (b, j, 0))
    rope_spec = pl.BlockSpec((tile, GROUP_LANES // 2), lambda b, j: (j, 0))
    (n1g, win, wqk, wcat, abias, avg, wpool, bscale, cng, wout) = weights
    (smask, vmask, cos_t, sin_t, d4, qdec, kdec, cdec, mask_k, mask_v, mask_s, mavg) = consts
    w = lambda a: (a, _resident_spec(a.shape, layer))
    k = lambda a: (a, _resident_spec(a.shape))
    operands = [(x, tile_spec), w(n1g), w(win), w(wqk), w(wcat), k(smask), w(abias), w(avg), k(vmask), w(wpool),
                w(bscale), (cos_t, rope_spec), (sin_t, rope_spec), k(d4), k(qdec), k(kdec), k(cdec),
                k(mask_k), k(mask_v), k(mask_s), k(mavg), w(cng), w(wout)]
    return pl.pallas_call(
        _mixer_kernel,
        grid=(batch, seq // tile),
        in_specs=[spec for _, spec in operands],
        out_specs=tile_spec,
        out_shape=jax.ShapeDtypeStruct(x.shape, x.dtype),
        scratch_shapes=[pltpu.VMEM((N_GROUPS, GROUP_LANES, GROUP_LANES), F32),
                        pltpu.VMEM((POOL_HALO, B_WIDTH), F32)],
        compiler_params=_compiler_params(),
        name="mixer",
    )(*[a for a, _ in operands])


def _ffn_call(x, layer, weights, final_g, apply_final_norm):
    batch, seq, _ = x.shape
    tile = TILE_FFN
    tile_spec = pl.BlockSpec((None, tile, D_MODEL), lambda b, j: (b, j, 0))
    in_specs = [tile_spec] + [_resident_spec(a.shape, layer) for a in weights] + [_resident_spec(final_g.shape)]
    return pl.pallas_call(
        functools.partial(_ffn_kernel, apply_final_norm=apply_final_norm),
        grid=(batch, seq // tile),
        in_specs=in_specs,
        out_specs=tile_spec,
        out_shape=jax.ShapeDtypeStruct(x.shape, x.dtype),
        scratch_shapes=[pltpu.VMEM((CONV_HALO, 2 * D_FF), F32)],
        compiler_params=_compiler_params(),
        name="ffn",
    )(x, *weights, final_g)


def kernel(x, norm1_g, w_in, a_vnorm_g, a_ws, a_bs, b_w, b_scale, c_norm_g, w_out, norm2_g, w_up,
           conv_w, conv_b, w_down, final_g):
    depth = w_in.shape[0]
    assert x.shape[1] == SEQ and x.shape[2] == D_MODEL
    smask, vmask = _spatial_constants()
    cos_t, sin_t = _rope_tables()
    d4, qdec, kdec, cdec, mask_k, mask_v, mask_s, mavg = _retention_constants()
    consts = (smask, vmask, cos_t, sin_t, d4, qdec, kdec, cdec, mask_k, mask_v, mask_s, mavg)
    rows = lambda a: a[:, None, :]
    n_pool = len(POOL_WINDOWS)
    pool_eye = jnp.eye(n_pool, dtype=F32)[None, :, None, :, None]
    w_in_bf = w_in.astype(BF16)
    qk_cols = slice(2 * A_WIDTH + B_WIDTH, 2 * A_WIDTH + B_WIDTH + 2 * C_WIDTH)
    mixer_weights = (
        rows(norm1_g),
        w_in_bf,
        _split_rotary_halves(w_in_bf[:, :, qk_cols]),
        a_ws.transpose(0, 2, 1, 3).reshape(depth, A_BLOCK, A_HEADS * A_BLOCK),
        jnp.repeat(a_bs.transpose(0, 2, 1), A_HEAD_DIM, axis=2),
        rows(a_vnorm_g),
        (b_w[:, :, :, None, :] * pool_eye).reshape(depth, B_WIDTH, B_WIDTH).astype(BF16),
        rows(b_scale),
        rows(c_norm_g),
        w_out.astype(BF16),
    )
    ffn_weights = (rows(norm2_g), w_up.astype(BF16), conv_w, rows(conv_b), w_down.astype(BF16))
    final_row = final_g.reshape(1, -1)
    for l in range(depth):
        x = _mixer_call(x, l, consts, mixer_weights)
        x = _ffn_call(x, l, ffn_weights, final_row, apply_final_norm=(l == depth - 1))
    return x
```

```python
import functools

import numpy as np
import jax
import jax.numpy as jnp
from jax import lax
from jax.experimental import pallas as pl
from jax.experimental.pallas import tpu as pltpu

D_MODEL = 1024
SEQ = 8192
CHUNK = 64
A_WIDTH = 256
A_HEADS = 4
A_HEAD_DIM = A_WIDTH // A_HEADS
A_BLOCK = 128
POOL_WINDOWS = (2, 4, 8, 16)
B_WIDTH = 256
B_GROUP_DIM = B_WIDTH // len(POOL_WINDOWS)
C_WIDTH = 512
C_HEADS = 8
C_HEAD_DIM = C_WIDTH // C_HEADS
ROPE_HALF = C_HEAD_DIM // 2
ROPE_BASE = 10000.0
IN_COLS = 2 * A_WIDTH + B_WIDTH + 4 * C_WIDTH
D_FF = 2816
CONV_WIDTH = 3
EPS = 1e-6

HEAD_GROUP = 4
GROUP_LANES = HEAD_GROUP * C_HEAD_DIM
N_GROUPS = C_HEADS // HEAD_GROUP
POOL_HALO = 16
CONV_HALO = 8

TILE_MIX = 1024
TILE_FFN = 1024
FF_CHUNK = 768
VMEM_LIMIT_BYTES = 56 * 1024 * 1024
SINGLE_BUFFER_MIN_ELEMS = 1 << 20
BF16_SUBLANES = 16

F32 = jnp.float32
BF16 = jnp.bfloat16


def _rms(x, g):
    return x * lax.rsqrt(jnp.mean(x * x, axis=-1, keepdims=True) + EPS) * g


def _qk_lane_head(lane):
    return (lane % (GROUP_LANES // 2)) // ROPE_HALF


def _split_rotary_halves(w_qk):
    n_layers = w_qk.shape[0]
    w = w_qk.reshape(n_layers, D_MODEL, 2, N_GROUPS, HEAD_GROUP, 2, ROPE_HALF)
    return w.transpose(0, 1, 2, 3, 5, 4, 6).reshape(n_layers, D_MODEL, 2 * C_WIDTH)


def _retention_constants():
    log_gamma = np.log1p(-np.exp2(-5.0 - np.arange(C_HEADS, dtype=np.float64)))
    idx = np.arange(CHUNK, dtype=np.float64)
    lane = np.arange(GROUP_LANES)
    d4 = np.zeros((N_GROUPS, CHUNK, GROUP_LANES))
    qdec = np.zeros((N_GROUPS, CHUNK, GROUP_LANES))
    kdec = np.zeros((N_GROUPS, CHUNK, GROUP_LANES))
    cdec = np.zeros((N_GROUPS, 1, GROUP_LANES))
    for gi in range(N_GROUPS):
        lg_v = log_gamma[gi * HEAD_GROUP + lane // C_HEAD_DIM]
        lg_qk = log_gamma[gi * HEAD_GROUP + _qk_lane_head(lane)]
        d4[gi] = np.exp(lg_v[None, :] * np.abs(idx[:, None] - (lane % CHUNK)[None, :]))
        qdec[gi] = np.exp(lg_qk[None, :] * (idx + 1)[:, None])
        kdec[gi] = np.exp(lg_qk[None, :] * (CHUNK - 1 - idx)[:, None])
        cdec[gi] = np.exp(lg_v * CHUNK)[None, :]
    rows_head = np.arange(GROUP_LANES) // CHUNK
    mask_k = (rows_head[:, None] == _qk_lane_head(lane)[None, :]).astype(np.float32)
    mask_v = (rows_head[:, None] == (lane // C_HEAD_DIM)[None, :]).astype(np.float32)
    f = lambda a: jnp.asarray(a, F32)
    return (f(d4), f(qdec), f(kdec), f(cdec), jnp.asarray(mask_k, BF16), jnp.asarray(mask_v, BF16),
            f(mask_k.T), jnp.asarray(mask_v / C_HEAD_DIM, BF16))


def _spatial_constants():
    chunk_id = np.arange(A_BLOCK) // CHUNK
    causal = (chunk_id[:, None] >= chunk_id[None, :]).astype(np.float32)
    smask = np.tile(causal, (1, A_HEADS))
    rows_head = np.arange(A_HEADS * A_BLOCK) // A_BLOCK
    vmask = (rows_head[:, None] == (np.arange(A_WIDTH) // A_HEAD_DIM)[None, :]).astype(np.float32)
    return jnp.asarray(smask), jnp.asarray(vmask, BF16)


def _rope_tables():
    inv = ROPE_BASE ** (-np.arange(ROPE_HALF, dtype=np.float64) / ROPE_HALF)
    ang = np.arange(SEQ, dtype=np.float64)[:, None] * inv[None, :]
    tab = lambda a: jnp.asarray(np.tile(a, (1, HEAD_GROUP)), F32)
    return tab(np.cos(ang)), tab(np.sin(ang))


def _mixer_kernel(x_ref, n1g_ref, win_ref, wqk_ref, wcat_ref, smask_ref, abias_ref, avg_ref, vmask_ref,
                  wpool_ref, bscale_ref, cos_ref, sin_ref, d4_ref, qdec_ref, kdec_ref, cdec_ref,
                  maskk_ref, maskv_ref, masks_ref, mavg_ref, cng_ref, wout_ref, wup_ref, wdown_ref,
                  out_ref, wup_bf_ref, wdown_bf_ref, state_ref, halo_ref):
    tile = x_ref.shape[0]
    j = pl.program_id(1)

    wup_bf_ref[...] = wup_ref[...].astype(BF16)
    wdown_bf_ref[...] = wdown_ref[...].astype(BF16)

    @pl.when(j == 0)
    def _():
        state_ref[...] = jnp.zeros_like(state_ref)
        halo_ref[...] = jnp.zeros_like(halo_ref)

    x = x_ref[...]
    h = _rms(x, n1g_ref[...]).astype(BF16)
    ab_cols = 2 * A_WIDTH + B_WIDTH
    proj_ab = jnp.dot(h, win_ref[:, :ab_cols], preferred_element_type=F32)
    proj_qk = jnp.dot(h, wqk_ref[...], preferred_element_type=F32)
    proj_vg = jnp.dot(h, win_ref[:, ab_cols + 2 * C_WIDTH:], preferred_element_type=F32)
    n_chunks = tile // CHUNK
    half = GROUP_LANES // 2

    za = jax.nn.gelu(proj_ab[:, :2 * A_WIDTH])
    u = za[:, :A_WIDTH]
    v = _rms(za[:, A_WIDTH:], avg_ref[...])
    wc = (wcat_ref[...] * smask_ref[...]).astype(BF16)
    vmask = vmask_ref[...]
    abias = abias_ref[...]
    sv = []
    for n in range(tile // A_BLOCK):
        vb = v[n * A_BLOCK:(n + 1) * A_BLOCK].astype(BF16)
        v4 = jnp.concatenate([vb] * A_HEADS, axis=0) * vmask
        sv.append(jnp.dot(wc, v4, preferred_element_type=F32) + abias)
    ya = u * jnp.concatenate(sv, axis=0)

    xb = proj_ab[:, 2 * A_WIDTH:]
    ext = jnp.concatenate([halo_ref[...], xb], axis=0)
    halo_ref[...] = xb[tile - POOL_HALO:]
    s2 = ext + pltpu.roll(ext, 1, 0)
    s4 = s2 + pltpu.roll(s2, 2, 0)
    s8 = s4 + pltpu.roll(s4, 4, 0)
    s16 = s8 + pltpu.roll(s8, 8, 0)
    grp = lax.broadcasted_iota(jnp.int32, (1, B_WIDTH), 1) // B_GROUP_DIM
    wsum = jnp.where(grp == 0, s2, jnp.where(grp == 1, s4, jnp.where(grp == 2, s8, s16)))[POOL_HALO:]
    win = jnp.where(grp == 0, 2, jnp.where(grp == 1, 4, jnp.where(grp == 2, 8, 16)))
    pos = j * tile + lax.broadcasted_iota(jnp.int32, (tile, 1), 0)
    cnt = jnp.minimum(pos + 1, win).astype(F32)
    pooled = wsum / cnt - xb
    yb = jnp.dot(pooled.astype(BF16), wpool_ref[...], preferred_element_type=F32) * bscale_ref[...]

    cosv = cos_ref[...]
    sinv = sin_ref[...]
    mask_k = maskk_ref[...]
    mask_v = maskv_ref[...]
    mask_s = masks_ref[...]
    mavg = mavg_ref[...]

    def rope(a):
        a1, a2 = a[:, :half], a[:, half:]
        return jnp.concatenate([a1 * cosv - a2 * sinv, a2 * cosv + a1 * sinv], axis=1)

    groups = []
    for gi in range(N_GROUPS):
        lanes = slice(gi * GROUP_LANES, (gi + 1) * GROUP_LANES)
        k_lanes = slice(C_WIDTH + gi * GROUP_LANES, C_WIDTH + (gi + 1) * GROUP_LANES)
        groups.append(dict(
            q=rope(proj_qk[:, lanes]) * (C_HEAD_DIM ** -0.5),
            k=rope(proj_qk[:, k_lanes]),
            v=proj_vg[:, lanes],
            gate=proj_vg[:, k_lanes],
            d4=d4_ref[gi], qdec=qdec_ref[gi], kdec=kdec_ref[gi], cdec=cdec_ref[gi],
            state=state_ref[gi], ys=[]))

    def prepare(g, c):
        rows = slice(c * CHUNK, (c + 1) * CHUNK)
        qc, kc, vc = g["q"][rows], g["k"][rows], g["v"][rows]
        vb = vc.astype(BF16)
        k4 = jnp.concatenate([kc.astype(BF16)] * HEAD_GROUP, axis=0) * mask_k
        v4 = jnp.concatenate([vb] * HEAD_GROUP, axis=0) * mask_v
        scores = lax.dot_general(qc.astype(BF16), k4, (((1,), (1,)), ((), ())),
                                 preferred_element_type=F32)
        return dict(scores=scores, v4=v4, qd=(qc * g["qdec"]).astype(BF16),
                    kd=(kc * g["kdec"]).astype(BF16), vb=vb)

    def state_contribution(ch):
        return lax.dot_general(ch["kd"], ch["vb"], (((0,), (0,)), ((), ())),
                               preferred_element_type=F32)

    cur = [prepare(g, 0) for g in groups]
    for ch in cur:
        ch["kv"] = state_contribution(ch)
    for c in range(n_chunks):
        nxt = [prepare(g, c + 1) for g in groups] if c + 1 < n_chunks else None
        for g, ch in zip(groups, cur):
            p = (ch["scores"] * g["d4"]).astype(BF16)
            y = jnp.dot(p, ch["v4"], preferred_element_type=F32)
            y = y + jnp.dot(ch["qd"], g["state"].astype(BF16), preferred_element_type=F32)
            g["ys"].append(y)
        if nxt is not None:
            for ch in nxt:
                ch["kv"] = state_contribution(ch)
        for g, ch in zip(groups, cur):
            g["state"] = g["state"] * g["cdec"] + ch["kv"] * mask_s
        cur = nxt
    for gi, g in enumerate(groups):
        state_ref[gi] = g["state"]

    y4 = [jnp.concatenate(g["ys"], axis=0) for g in groups]
    mu = []
    for y in y4:
        y_hi = y.astype(BF16)
        y_lo = (y - y_hi.astype(F32)).astype(BF16)
        mu.append(jnp.dot(y_hi, mavg, preferred_element_type=F32)
                  + jnp.dot(y_lo, mavg, preferred_element_type=F32))
    dev = [y - m for y, m in zip(y4, mu)]
    var = [jnp.dot((d * d).astype(BF16), mavg, preferred_element_type=F32) for d in dev]
    yc = []
    for gi, g in enumerate(groups):
        yn = dev[gi] * lax.rsqrt(var[gi] + EPS) * cng_ref[:, gi * GROUP_LANES:(gi + 1) * GROUP_LANES]
        yc.append(jax.nn.silu(g["gate"]) * yn)

    mix = jnp.concatenate([ya, yb] + yc, axis=1).astype(BF16)
    out_ref[...] = x + jnp.dot(mix, wout_ref[...], preferred_element_type=F32)


def _ffn_kernel(x_ref, n2g_ref, wup_ref, convw_ref, convb_ref, wdown_ref, fg_ref, out_ref, halo_ref,
                *, apply_final_norm):
    tile = x_ref.shape[0]
    j = pl.program_id(1)

    @pl.when(j == 0)
    def _():
        halo_ref[...] = jnp.zeros_like(halo_ref)

    x = x_ref[...]
    h = _rms(x, n2g_ref[...]).astype(BF16)

    bounds = list(range(0, D_FF, FF_CHUNK)) + [D_FF]
    spans = list(zip(bounds[:-1], bounds[1:]))
    n_chunks = len(spans)

    def up_proj(c):
        lo, hi = spans[c]
        return [jnp.dot(h, wup_ref[:, part * D_FF + lo:part * D_FF + hi], preferred_element_type=F32)
                for part in range(2)]

    acts = []
    nxt = up_proj(0)
    for c in range(n_chunks):
        cur = nxt
        if c + 1 < n_chunks:
            nxt = up_proj(c + 1)
        conv = []
        for part in range(2):
            cols = slice(part * D_FF + spans[c][0], part * D_FF + spans[c][1])
            up = cur[part]
            ext = jnp.concatenate([halo_ref[:, cols], up], axis=0)
            halo_ref[:, cols] = up[tile - CONV_HALO:]
            up1 = pltpu.roll(ext, 1, 0)[CONV_HALO:]
            up2 = pltpu.roll(ext, 2, 0)[CONV_HALO:]
            w = convw_ref[:, cols]
            conv.append(convb_ref[:, cols] + up2 * w[0:1] + up1 * w[1:2] + up * w[2:3])
        acts.append((jax.nn.silu(conv[0]) * conv[1]).astype(BF16))
    act = jnp.concatenate(acts, axis=1)
    acc = x + jnp.dot(act, wdown_ref[...], preferred_element_type=F32)
    if apply_final_norm:
        acc = _rms(acc, fg_ref[...])
    out_ref[...] = acc


def _resident_spec(shape, layer=None):
    if layer is None:
        block, index = tuple(shape), (0,) * len(shape)
    else:
        block, index = (None,) + tuple(shape[1:]), (layer,) + (0,) * (len(shape) - 1)
    if int(np.prod(shape[(layer is not None):])) >= SINGLE_BUFFER_MIN_ELEMS:
        return pl.BlockSpec(block, lambda b, j: index, pipeline_mode=pl.Buffered(1))
    return pl.BlockSpec(block, lambda b, j: index)


def _compiler_params():
    return pltpu.CompilerParams(dimension_semantics=("arbitrary", "arbitrary"),
                                vmem_limit_bytes=VMEM_LIMIT_BYTES)


def _row_block_specs(rows, cols, layer, n_steps, n_j):
    n_blocks = max(n for n in range(1, n_steps + 1) if rows % (n * BF16_SUBLANES) == 0)
    block_rows = rows // n_blocks
    step = lambda b, j: jnp.minimum(b * n_j + j, n_blocks - 1)
    return (pl.BlockSpec((None, block_rows, cols), lambda b, j: (layer, step(b, j), 0)),
            pl.BlockSpec((block_rows, cols), lambda b, j: (step(b, j), 0)))


def _mixer_call(x, layer, consts, weights, w_up, w_down):
    batch, seq, _ = x.shape
    tile = TILE_MIX
    n_j = seq // tile
    tile_spec = pl.BlockSpec((None, tile, D_MODEL), lambda b, j: (b, j, 0))
    rope_spec = pl.BlockSpec((tile, GROUP_LANES // 2), lambda b, j: (j, 0))
    wup_in, wup_out = _row_block_specs(w_up.shape[1], w_up.shape[2], layer, batch * n_j, n_j)
    wdown_in, wdown_out = _row_block_specs(w_down.shape[1], w_down.shape[2], layer, batch * n_j, n_j)
    (n1g, win, wqk, wcat, abias, avg, wpool, bscale, cng, wout) = weights
    (smask, vmask, cos_t, sin_t, d4, qdec, kdec, cdec, mask_k, mask_v, mask_s, mavg) = consts
    w = lambda a: (a, _resident_spec(a.shape, layer))
    k = lambda a: (a, _resident_spec(a.shape))
    operands = [(x, tile_spec), w(n1g), w(win), w(wqk), w(wcat), k(smask), w(abias), w(avg), k(vmask), w(wpool),
                w(bscale), (cos_t, rope_spec), (sin_t, rope_spec), k(d4), k(qdec), k(kdec), k(cdec),
                k(mask_k), k(mask_v), k(mask_s), k(mavg), w(cng), w(wout), (w_up, wup_in), (w_down, wdown_in)]
    return pl.pallas_call(
        _mixer_kernel,
        grid=(batch, n_j),
        in_specs=[spec for _, spec in operands],
        out_specs=[tile_spec, wup_out, wdown_out],
        out_shape=[jax.ShapeDtypeStruct(x.shape, x.dtype),
                   jax.ShapeDtypeStruct(w_up.shape[1:], BF16),
                   jax.ShapeDtypeStruct(w_down.shape[1:], BF16)],
        scratch_shapes=[pltpu.VMEM((N_GROUPS, GROUP_LANES, GROUP_LANES), F32),
                        pltpu.VMEM((POOL_HALO, B_WIDTH), F32)],
        compiler_params=_compiler_params(),
        name="mixer",
    )(*[a for a, _ in operands])


def _ffn_call(x, layer, weights, final_g, apply_final_norm):
    batch, seq, _ = x.shape
    tile = TILE_FFN
    tile_spec = pl.BlockSpec((None, tile, D_MODEL), lambda b, j: (b, j, 0))
    in_specs = ([tile_spec]
                + [_resident_spec(a.shape, layer if a.ndim == 3 else None) for a in weights]
                + [_resident_spec(final_g.shape)])
    return pl.pallas_call(
        functools.partial(_ffn_kernel, apply_final_norm=apply_final_norm),
        grid=(batch, seq // tile),
        in_specs=in_specs,
        out_specs=tile_spec,
        out_shape=jax.ShapeDtypeStruct(x.shape, x.dtype),
        scratch_shapes=[pltpu.VMEM((CONV_HALO, 2 * D_FF), F32)],
        compiler_params=_compiler_params(),
        name="ffn",
    )(x, *weights, final_g)


def kernel(x, norm1_g, w_in, a_vnorm_g, a_ws, a_bs, b_w, b_scale, c_norm_g, w_out, norm2_g, w_up,
           conv_w, conv_b, w_down, final_g):
    depth = w_in.shape[0]
    assert x.shape[1] == SEQ and x.shape[2] == D_MODEL
    smask, vmask = _spatial_constants()
    cos_t, sin_t = _rope_tables()
    d4, qdec, kdec, cdec, mask_k, mask_v, mask_s, mavg = _retention_constants()
    consts = (smask, vmask, cos_t, sin_t, d4, qdec, kdec, cdec, mask_k, mask_v, mask_s, mavg)
    rows = lambda a: a[:, None, :]
    n_pool = len(POOL_WINDOWS)
    pool_eye = jnp.eye(n_pool, dtype=F32)[None, :, None, :, None]
    w_in_bf = w_in.astype(BF16)
    qk_cols = slice(2 * A_WIDTH + B_WIDTH, 2 * A_WIDTH + B_WIDTH + 2 * C_WIDTH)
    mixer_weights = (
        rows(norm1_g),
        w_in_bf,
        _split_rotary_halves(w_in_bf[:, :, qk_cols]),
        a_ws.transpose(0, 2, 1, 3).reshape(depth, A_BLOCK, A_HEADS * A_BLOCK),
        jnp.repeat(a_bs.transpose(0, 2, 1), A_HEAD_DIM, axis=2),
        rows(a_vnorm_g),
        (b_w[:, :, :, None, :] * pool_eye).reshape(depth, B_WIDTH, B_WIDTH).astype(BF16),
        rows(b_scale),
        rows(c_norm_g),
        w_out.astype(BF16),
    )
    final_row = final_g.reshape(1, -1)
    for l in range(depth):
        x, w_up_bf, w_down_bf = _mixer_call(x, l, consts, mixer_weights, w_up, w_down)
        ffn_weights = (rows(norm2_g), w_up_bf, conv_w, rows(conv_b), w_down_bf)
        x = _ffn_call(x, l, ffn_weights, final_row, apply_final_norm=(l == depth - 1))
    return x
```
